```python
import jax, jax.numpy as jnp
from jax import lax
import numpy as np

D_MODEL = 4096
BATCH = 4
SEQ = 4096
DEPTH = 1
DEC_BATCH = 4
DEC_SEQ = 2048
PAST_LEN = 128

GRID_W = 64
D_MIX = D_MODEL
D_ATTN = D_MIX // 2
D_CONV = D_MIX - D_ATTN
HEAD_DIM = 128
N_ATTN_HEADS = D_ATTN // HEAD_DIM
WIN_ROWS_MAX = 8
WIN_COLS = 16
CONV_WIDTH = 31
D_FF = -(-8 * D_MODEL // (3 * 256)) * 256
D_IN = 3 * D_ATTN + 2 * D_CONV
N_MOD = 6
EPS = 1e-6
LN_EPS = 1e-5

kernel_name = "hymba_natten_conformer_adaln_encoder"


def _rmsnorm(x, g):
    x32 = x.astype(jnp.float32)
    y = x32 * lax.rsqrt(jnp.mean(x32 * x32, axis=-1, keepdims=True) + EPS)
    return (y * g.astype(jnp.float32)).astype(x.dtype)


def _neighbourhood_attention(q, k, v, rpb):
    B, S, _ = q.shape
    rows = S // GRID_W
    kr = min(WIN_ROWS_MAX, rows)
    shp = (B, rows, GRID_W, N_ATTN_HEADS, HEAD_DIM)
    qg, kg, vg = q.reshape(shp), k.reshape(shp), v.reshape(shp)
    cols = jnp.arange(GRID_W)
    col_start = jnp.clip(cols - WIN_COLS // 2, 0, GRID_W - WIN_COLS)
    col_idx = col_start[:, None] + jnp.arange(WIN_COLS)[None, :]
    col_off = col_idx - cols[:, None] + (WIN_COLS - 1)
    scale = HEAD_DIM ** -0.5

    def row_block(r):
        rs = jnp.clip(r - kr // 2, 0, rows - kr)
        q_r = lax.dynamic_index_in_dim(qg, r, axis=1, keepdims=False)
        k_band = lax.dynamic_slice_in_dim(kg, rs, kr, axis=1)
        v_band = lax.dynamic_slice_in_dim(vg, rs, kr, axis=1)
        k_win = k_band[:, :, col_idx]
        v_win = v_band[:, :, col_idx]
        row_off = rs + jnp.arange(kr) - r + (WIN_ROWS_MAX - 1)
        bias = rpb[:, row_off[:, None, None], col_off[None, :, :]]
        s = (jnp.einsum('bqhd,biqjhd->bhiqj', q_r, k_win).astype(jnp.float32) * scale
             + bias.astype(jnp.float32)[None])
        p = jax.nn.softmax(s, axis=(2, 4)).astype(v.dtype)
        return jnp.einsum('bhiqj,biqjhd->bqhd', p, v_win)

    out = lax.map(row_block, jnp.arange(rows))
    return jnp.moveaxis(out, 0, 1).reshape(B, S, D_ATTN)


def _conformer_conv(a, g, w_dw, b_dw, ln_g, ln_b):
    u = a * jax.nn.sigmoid(g)
    pad = CONV_WIDTH // 2
    u = lax.conv_general_dilated(
        u, w_dw.astype(u.dtype)[:, None, :], window_strides=(1,), padding=[(pad, pad)],
        dimension_numbers=('NWC', 'WIO', 'NWC'), feature_group_count=D_CONV) + b_dw.astype(u.dtype)
    u32 = u.astype(jnp.float32)
    mu = jnp.mean(u32, axis=-1, keepdims=True)
    var = jnp.mean(jnp.square(u32 - mu), axis=-1, keepdims=True)
    un = (u32 - mu) * lax.rsqrt(var + LN_EPS) * ln_g.astype(jnp.float32) + ln_b.astype(jnp.float32)
    return jax.nn.silu(un).astype(a.dtype)


def _encoder(x, c, w_ada, b_ada, g_mix, w_in, rpb, w_dw, b_dw, ln_g, ln_b, w_o,
             g_ffn, w_gate, w_up, w_down, g_final):
    B = x.shape[0]
    for l in range(DEPTH):
        mod = (jax.nn.silu(c) @ w_ada[l] + b_ada[l]).reshape(B, N_MOD, D_MODEL)
        sh1, sc1, gt1, sh2, sc2, gt2 = [mod[:, i, None, :] for i in range(N_MOD)]
        h = _rmsnorm(x, g_mix[l]) * (1 + sc1) + sh1
        proj = h @ w_in[l]
        q, k, v, a, gg = jnp.split(
            proj, [D_ATTN, 2 * D_ATTN, 3 * D_ATTN, 3 * D_ATTN + D_CONV], axis=-1)
        y_attn = _neighbourhood_attention(q, k, v, rpb[l])
        y_conv = _conformer_conv(a, gg, w_dw[l], b_dw[l], ln_g[l], ln_b[l])
        y = jnp.concatenate([y_attn, y_conv], axis=-1) @ w_o[l]
        x = x + gt1 * y
        h = _rmsnorm(x, g_ffn[l]) * (1 + sc2) + sh2
        f = (jax.nn.silu(h @ w_gate[l]) * (h @ w_up[l])) @ w_down[l]
        x = x + gt2 * f
    return _rmsnorm(x, g_final)


def setup_inputs(seed: int = 0) -> dict:
    key = jax.random.key(seed)
    ks = jax.random.split(key, 20)
    f32 = jnp.float32
    nrm = lambda k, shp, s: jax.random.normal(k, shp, f32) * s
    return {
        "x_prompt": nrm(ks[0], (BATCH, SEQ, D_MODEL), 1.0),
        "x_sample": nrm(ks[1], (DEC_BATCH, DEC_SEQ, D_MODEL), 1.0),
        "c_prompt": nrm(ks[2], (BATCH, D_MODEL), 1.0),
        "c_sample": nrm(ks[3], (DEC_BATCH, D_MODEL), 1.0),
        "w_ada": nrm(ks[4], (DEPTH, D_MODEL, N_MOD * D_MODEL), 0.5 * D_MODEL ** -0.5),
        "b_ada": nrm(ks[5], (DEPTH, N_MOD * D_MODEL), 0.02),
        "g_mix": 1.0 + nrm(ks[6], (DEPTH, D_MODEL), 0.02),
        "w_in": nrm(ks[7], (DEPTH, D_MODEL, D_IN), D_MODEL ** -0.5),
        "rpb": nrm(ks[8], (DEPTH, N_ATTN_HEADS, 2 * WIN_ROWS_MAX - 1, 2 * WIN_COLS - 1), 0.1),
        "w_dw": nrm(ks[9], (DEPTH, CONV_WIDTH, D_CONV), CONV_WIDTH ** -0.5),
        "b_dw": nrm(ks[10], (DEPTH, D_CONV), 0.02),
        "ln_g": 1.0 + nrm(ks[11], (DEPTH, D_CONV), 0.02),
        "ln_b": nrm(ks[12], (DEPTH, D_CONV), 0.02),
        "w_o": nrm(ks[13], (DEPTH, D_MIX, D_MODEL), D_MIX ** -0.5),
        "g_ffn": 1.0 + nrm(ks[14], (DEPTH, D_MODEL), 0.02),
        "w_gate": nrm(ks[15], (DEPTH, D_MODEL, D_FF), D_MODEL ** -0.5),
        "w_up": nrm(ks[16], (DEPTH, D_MODEL, D_FF), D_MODEL ** -0.5),
        "w_down": nrm(ks[17], (DEPTH, D_FF, D_MODEL), D_FF ** -0.5),
        "g_final": 1.0 + nrm(ks[18], (D_MODEL,), 0.02),
    }


def reference(x_prompt, x_sample, c_prompt, c_sample, w_ada, b_ada, g_mix, w_in, rpb,
              w_dw, b_dw, ln_g, ln_b, w_o, g_ffn, w_gate, w_up, w_down, g_final):
    y_prompt = _encoder(x_prompt, c_prompt, w_ada, b_ada, g_mix, w_in, rpb, w_dw, b_dw,
                        ln_g, ln_b, w_o, g_ffn, w_gate, w_up, w_down, g_final)
    y_sample = _encoder(x_sample, c_sample, w_ada, b_ada, g_mix, w_in, rpb, w_dw, b_dw,
                        ln_g, ln_b, w_o, g_ffn, w_gate, w_up, w_down, g_final)
    return (y_prompt, y_sample)
```

```python
import functools

import jax
import jax.numpy as jnp
from jax import lax
from jax.experimental import pallas as pl
from jax.experimental.pallas import tpu as pltpu

F32 = jnp.float32
BF16 = jnp.bfloat16

D_MODEL = 4096
GRID_W = 64
D_ATTN = D_MODEL // 2
D_CONV = D_MODEL - D_ATTN
HEAD_DIM = 128
N_HEADS = D_ATTN // HEAD_DIM
WIN_ROWS = 8
WIN_COLS = 16
CONV_WIDTH = 31
CONV_PAD = CONV_WIDTH // 2
D_FF = 11008
N_MOD = 6
EPS = 1e-6
LN_EPS = 1e-5

LANES = 128
VMEM_BYTES_V7X = 64 * 1024 * 1024
VMEM_LIMIT = VMEM_BYTES_V7X - 8 * 1024 * 1024
FF_TILE = 1024
D_FF_PAD = -(-D_FF // FF_TILE) * FF_TILE

N_RPB_ROWS = 2 * WIN_ROWS - 1
N_RPB_COLS = 2 * WIN_COLS - 1
N_BIAS_PAIRS = N_RPB_ROWS - 1
MASKED = -1e30
HALO = 16
N_CCHUNK = D_CONV // LANES


def _params(sem, est_bytes):
    return pltpu.CompilerParams(dimension_semantics=sem, vmem_limit_bytes=VMEM_LIMIT)


def _row_tile(s, cap):
    t = min(s, cap)
    assert s % t == 0
    return t


def _mod_kernel(c_ref, w_ref, b_ref, o_ref):
    c = c_ref[...]
    s = (c * jax.nn.sigmoid(c)).astype(BF16)
    w = w_ref[...].astype(BF16)
    o_ref[...] = jnp.dot(s, w, preferred_element_type=F32) + b_ref[...]


def adaln_mod(c, w, b):
    nb, d = c.shape
    n = w.shape[1]
    tn = 512
    return pl.pallas_call(
        _mod_kernel,
        grid=(n // tn,),
        in_specs=[
            pl.BlockSpec((nb, d), lambda j: (0, 0)),
            pl.BlockSpec((d, tn), lambda j: (0, j)),
            pl.BlockSpec((1, tn), lambda j: (0, j)),
        ],
        out_specs=pl.BlockSpec((nb, tn), lambda j: (0, j)),
        out_shape=jax.ShapeDtypeStruct((nb, n), F32),
        compiler_params=_params(("parallel",), 0),
        name="adaln_mod",
    )(c, w, b.reshape(1, n))


def _prenorm_kernel(x_ref, g_ref, mod_ref, o_ref, *, shift_idx, scale_idx):
    x = x_ref[0]
    ms = jnp.mean(x * x, axis=-1, keepdims=True)
    y = (x * lax.rsqrt(ms + EPS)) * g_ref[...]
    if scale_idx is not None:
        y = y * (1.0 + mod_ref[0, scale_idx:scale_idx + 1, :]) + mod_ref[0, shift_idx:shift_idx + 1, :]
    o_ref[0] = y.astype(o_ref.dtype)


def prenorm(x, g, mod3, boff, shift_idx, scale_idx, out_dtype):
    b, s, d = x.shape
    tr = _row_tile(s, 256)
    return pl.pallas_call(
        functools.partial(_prenorm_kernel, shift_idx=shift_idx, scale_idx=scale_idx),
        grid=(b, s // tr),
        in_specs=[
            pl.BlockSpec((1, tr, d), lambda bi, i: (bi, i, 0)),
            pl.BlockSpec((1, d), lambda bi, i: (0, 0)),
            pl.BlockSpec((1, N_MOD, d), lambda bi, i: (bi + boff, 0, 0)),
        ],
        out_specs=pl.BlockSpec((1, tr, d), lambda bi, i: (bi, i, 0)),
        out_shape=jax.ShapeDtypeStruct((b, s, d), out_dtype),
        compiler_params=_params(("parallel", "parallel"), 0),
        name="prenorm",
    )(x, g.reshape(1, d), mod3)


def _mm_kernel(x_ref, w_ref, o_ref):
    o_ref[0] = jnp.dot(x_ref[0], w_ref[...], preferred_element_type=F32).astype(o_ref.dtype)


def mm_in(h, w):
    b, s, k = h.shape
    n = w.shape[1]
    tm = _row_tile(s, 1024)
    tn = 1024
    return pl.pallas_call(
        _mm_kernel,
        grid=(b, s // tm, n // tn),
        in_specs=[
            pl.BlockSpec((1, tm, k), lambda bi, i, j: (bi, i, 0)),
            pl.BlockSpec((k, tn), lambda bi, i, j: (0, j)),
        ],
        out_specs=pl.BlockSpec((1, tm, tn), lambda bi, i, j: (bi, i, j)),
        out_shape=jax.ShapeDtypeStruct((b, s, n), BF16),
        compiler_params=_params(("parallel", "parallel", "parallel"), 0),
        name="mm_in",
    )(h, w)


def _attn_kernel(rpb_ref, q_ref, k_ref, v_ref, o_ref, bias_scr, *, rows):
    h = pl.program_id(1)
    base = h * (N_RPB_ROWS * N_RPB_COLS)
    lane = lax.broadcasted_iota(jnp.int32, (8, 2 * GRID_W), 1)
    sub = lax.broadcasted_iota(jnp.int32, (8, 2 * GRID_W), 0)
    kc = lane & (GRID_W - 1)
    for strip in range(GRID_W // 8):
        qc = sub + 8 * strip
        d = kc - qc + (WIN_COLS - 1)
        cs = jnp.clip(qc - WIN_COLS // 2, 0, GRID_W - WIN_COLS)
        in_win = (kc >= cs) & (kc < cs + WIN_COLS)
        tiles = [jnp.zeros((8, 2 * GRID_W), F32) for _ in range(N_RPB_ROWS)]
        for j in range(N_RPB_COLS):
            hit = d == j
            for i in range(N_RPB_ROWS):
                tiles[i] = jnp.where(hit, rpb_ref[base + i * N_RPB_COLS + j], tiles[i])
        for i in range(N_BIAS_PAIRS):
            pair = jnp.where(lane < GRID_W, tiles[i], tiles[i + 1])
            bias_scr[i, 8 * strip:8 * strip + 8, :] = jnp.where(in_win, pair, MASKED)

    scale = HEAD_DIM ** -0.5
    band = WIN_ROWS * GRID_W

    def body(r, carry):
        rs = jnp.clip(r - WIN_ROWS // 2, 0, rows - WIN_ROWS)
        first = (WIN_ROWS - 1) - (r - rs)
        q = q_ref[0, pl.ds(pl.multiple_of(r * GRID_W, GRID_W), GRID_W), :]
        k0 = pl.multiple_of(rs * GRID_W, GRID_W)
        kb = k_ref[0, pl.ds(k0, band), :]
        vb = v_ref[0, pl.ds(k0, band), :]
        s = lax.dot_general(q, kb, (((1,), (1,)), ((), ())), preferred_element_type=F32)
        bias = jnp.concatenate([bias_scr[first + 2 * m] for m in range(WIN_ROWS // 2)], axis=1)
        s = s * scale + bias
        m = jnp.max(s, axis=1, keepdims=True)
        p = jnp.exp(s - m)
        l = jnp.sum(p, axis=1, keepdims=True)
        o = jnp.dot(p.astype(BF16), vb, preferred_element_type=F32)
        o_ref[0, pl.ds(pl.multiple_of(r * GRID_W, GRID_W), GRID_W), :] = (o / l).astype(o_ref.dtype)
        return carry

    lax.fori_loop(0, rows, body, 0)


def nbr_attn(proj, rpb_flat):
    b, s, _ = proj.shape
    rows = s // GRID_W
    assert rows >= WIN_ROWS
    blk = (1, s, HEAD_DIM)
    return pl.pallas_call(
        functools.partial(_attn_kernel, rows=rows),
        grid=(b, N_HEADS),
        in_specs=[
            pl.BlockSpec(memory_space=pltpu.SMEM),
            pl.BlockSpec(blk, lambda bi, h: (bi, 0, h)),
            pl.BlockSpec(blk, lambda bi, h: (bi, 0, N_HEADS + h)),
            pl.BlockSpec(blk, lambda bi, h: (bi, 0, 2 * N_HEADS + h)),
        ],
        out_specs=pl.BlockSpec(blk, lambda bi, h: (bi, 0, h)),
        out_shape=jax.ShapeDtypeStruct((b, s, D_ATTN + D_CONV), BF16),
        scratch_shapes=[pltpu.VMEM((N_BIAS_PAIRS, GRID_W, 2 * GRID_W), F32)],
        compiler_params=_params(("parallel", "parallel"), 0),
        name="nbr_attn",
    )(rpb_flat, proj, proj, proj)


def _conv_kernel(a_ref, ap_ref, an_ref, g_ref, gp_ref, gn_ref, w_ref, bdw_ref, lng_ref, lnb_ref, ycat_ref,
                 o_ref, u_scr, c_scr, *, t_rows, n_blk, row_chunk):
    del ycat_ref
    i = pl.program_id(1)

    def glu(a, g):
        return a.astype(F32) * jax.nn.sigmoid(g.astype(F32))

    keep_prev = (i > 0).astype(F32)
    keep_next = (i < n_blk - 1).astype(F32)
    for c in range(N_CCHUNK):
        cols = slice(c * LANES, (c + 1) * LANES)
        u_scr[c, 0:HALO, :] = glu(ap_ref[0, :, cols], gp_ref[0, :, cols]) * keep_prev
        u_scr[c, HALO:HALO + t_rows, :] = glu(a_ref[0, :, cols], g_ref[0, :, cols])
        u_scr[c, HALO + t_rows:2 * HALO + t_rows, :] = glu(an_ref[0, :, cols], gn_ref[0, :, cols]) * keep_next

    def conv_chunk(c, carry):
        w = w_ref[c]
        bias = bdw_ref[c]
        for rc in range(t_rows // row_chunk):
            acc = jnp.zeros((row_chunk, LANES), F32)
            for j in range(CONV_WIDTH):
                start = rc * row_chunk + HALO - CONV_PAD + j
                acc = acc + u_scr[c, start:start + row_chunk, :] * w[j:j + 1, :]
            c_scr[c, rc * row_chunk:(rc + 1) * row_chunk, :] = acc + bias
        return carry

    lax.fori_loop(0, N_CCHUNK, conv_chunk, 0)

    tot = c_scr[0]
    for c in range(1, N_CCHUNK):
        tot = tot + c_scr[c]
    mu = jnp.sum(tot, axis=1, keepdims=True) * (1.0 / D_CONV)
    sq = jnp.zeros((t_rows, LANES), F32)
    for c in range(N_CCHUNK):
        dlt = c_scr[c] - mu
        sq = sq + dlt * dlt
    var = jnp.sum(sq, axis=1, keepdims=True) * (1.0 / D_CONV)
    rstd = lax.rsqrt(var + LN_EPS)
    for c in range(N_CCHUNK):
        yn = (c_scr[c] - mu) * rstd * lng_ref[c] + lnb_ref[c]
        o_ref[0, :, c * LANES:(c + 1) * LANES] = (yn * jax.nn.sigmoid(yn)).astype(o_ref.dtype)


def conv_glu_ln(proj, ycat, w_dw, b_dw, ln_g, ln_b):
    b, s, _ = proj.shape
    t_rows = _row_tile(s, 256)
    n_blk = s // t_rows
    hpb = t_rows // HALO
    a_col = 3 * D_ATTN // D_CONV
    g_col = a_col + 1
    main = (1, t_rows, D_CONV)
    halo = (1, HALO, D_CONV)

    def prev_map(col):
        return lambda bi, i: (bi, jnp.maximum(i * hpb - 1, 0), col)

    def next_map(col):
        return lambda bi, i: (bi, jnp.minimum((i + 1) * hpb, s // HALO - 1), col)

    w_chunks = jnp.pad(w_dw, ((0, 1), (0, 0))).reshape(CONV_WIDTH + 1, N_CCHUNK, LANES).transpose(1, 0, 2)
    per_chunk = lambda v: v.reshape(N_CCHUNK, 1, LANES)
    small = pl.BlockSpec((N_CCHUNK, 1, LANES), lambda bi, i: (0, 0, 0))
    return pl.pallas_call(
        functools.partial(_conv_kernel, t_rows=t_rows, n_blk=n_blk, row_chunk=64),
        grid=(b, n_blk),
        in_specs=[
            pl.BlockSpec(main, lambda bi, i: (bi, i, a_col)),
            pl.BlockSpec(halo, prev_map(a_col)),
            pl.BlockSpec(halo, next_map(a_col)),
            pl.BlockSpec(main, lambda bi, i: (bi, i, g_col)),
            pl.BlockSpec(halo, prev_map(g_col)),
            pl.BlockSpec(halo, next_map(g_col)),
            pl.BlockSpec((N_CCHUNK, CONV_WIDTH + 1, LANES), lambda bi, i: (0, 0, 0)),
            small, small, small,
            pl.BlockSpec(memory_space=pl.ANY),
        ],
        out_specs=pl.BlockSpec(main, lambda bi, i: (bi, i, D_ATTN // D_CONV)),
        out_shape=jax.ShapeDtypeStruct(ycat.shape, ycat.dtype),
        scratch_shapes=[
            pltpu.VMEM((N_CCHUNK, t_rows + 2 * HALO, LANES), F32),
            pltpu.VMEM((N_CCHUNK, t_rows, LANES), F32),
        ],
        input_output_aliases={10: 0},
        compiler_params=_params(("parallel", "parallel"), 0),
        name="conv_glu_ln",
    )(proj, proj, proj, proj, proj, proj, w_chunks, per_chunk(b_dw), per_chunk(ln_g), per_chunk(ln_b), ycat)


def _mm_res_kernel(a_ref, w_ref, x_ref, mod_ref, o_ref, acc_scr, *, gate_idx, nk):
    k = pl.program_id(3)

    @pl.when(k == 0)
    def _():
        acc_scr[...] = jnp.zeros_like(acc_scr)

    acc_scr[...] += jnp.dot(a_ref[0], w_ref[...], preferred_element_type=F32)

    @pl.when(k == nk - 1)
    def _():
        o_ref[0] = x_ref[0] + mod_ref[0, gate_idx:gate_idx + 1, :] * acc_scr[...]


def mm_res(a, w, x, mod3, boff, gate_idx):
    b, s, kd = a.shape
    n = w.shape[1]
    tm = _row_tile(s, 1024)
    tn = 1024
    tk = 1024
    nk = kd // tk
    return pl.pallas_call(
        functools.partial(_mm_res_kernel, gate_idx=gate_idx, nk=nk),
        grid=(b, s // tm, n // tn, nk),
        in_specs=[
            pl.BlockSpec((1, tm, tk), lambda bi, i, j, k: (bi, i, k)),
            pl.BlockSpec((tk, tn), lambda bi, i, j, k: (k, j)),
            pl.BlockSpec((1, tm, tn), lambda bi, i, j, k: (bi, i, j)),
            pl.BlockSpec((1, N_MOD, tn), lambda bi, i, j, k: (bi + boff, 0, j)),
        ],
        out_specs=pl.BlockSpec((1, tm, tn), lambda bi, i, j, k: (bi, i, j)),
        out_shape=jax.ShapeDtypeStruct((b, s, n), F32),
        scratch_shapes=[pltpu.VMEM((tm, tn), F32)],
        compiler_params=_params(("parallel", "parallel", "parallel", "arbitrary"), 0),
        name="mm_res",
    )(a, w, x, mod3)


def _ffn_up_kernel(h_ref, wg_ref, wu_ref, o_ref):
    h = h_ref[0]
    g = jnp.dot(h, wg_ref[...], preferred_element_type=F32)
    u = jnp.dot(h, wu_ref[...], preferred_element_type=F32)
    o_ref[0] = ((g * jax.nn.sigmoid(g)) * u).astype(o_ref.dtype)


def ffn_up(h, wg, wu):
    b, s, k = h.shape
    n = wg.shape[1]
    tm = _row_tile(s, 1024)
    tn = 512
    return pl.pallas_call(
        _ffn_up_kernel,
        grid=(b, s // tm, n // tn),
        in_specs=[
            pl.BlockSpec((1, tm, k), lambda bi, i, j: (bi, i, 0)),
            pl.BlockSpec((k, tn), lambda bi, i, j: (0, j)),
            pl.BlockSpec((k, tn), lambda bi, i, j: (0, j)),
        ],
        out_specs=pl.BlockSpec((1, tm, tn), lambda bi, i, j: (bi, i, j)),
        out_shape=jax.ShapeDtypeStruct((b, s, n), BF16),
        compiler_params=_params(("parallel", "parallel", "parallel"), 0),
        name="ffn_up",
    )(h, wg, wu)


def _encoder(x, boff, mod3, g_mix, w_in, rpb_flat, w_dw, b_dw, ln_g, ln_b, w_o, g_ffn, w_gate, w_up, w_down,
             g_final):
    h = prenorm(x, g_mix, mod3, boff, 0, 1, BF16)
    proj = mm_in(h, w_in)
    ycat = nbr_attn(proj, rpb_flat)
    ycat = conv_glu_ln(proj, ycat, w_dw, b_dw, ln_g, ln_b)
    x = mm_res(ycat, w_o, x, mod3, boff, 2)
    h = prenorm(x, g_ffn, mod3, boff, 3, 4, BF16)
    act = ffn_up(h, w_gate, w_up)
    x = mm_res(act, w_down, x, mod3, boff, 5)
    return prenorm(x, g_final, mod3, boff, None, None, F32)


def kernel(x_prompt, x_sample, c_prompt, c_sample, w_ada, b_ada, g_mix, w_in, rpb, w_dw, b_dw, ln_g, ln_b, w_o,
           g_ffn, w_gate, w_up, w_down, g_final):
    assert w_ada.shape[0] == 1, "single-layer block"
    n_prompt = c_prompt.shape[0]
    c_all = jnp.concatenate([c_prompt, c_sample], axis=0)
    mod = adaln_mod(c_all, w_ada[0], b_ada[0])
    mod3 = mod.reshape(c_all.shape[0], N_MOD, D_MODEL)

    ff_pad = D_FF_PAD - D_FF
    weights = dict(
        g_mix=g_mix[0],
        w_in=w_in[0].astype(BF16),
        rpb_flat=rpb[0].reshape(-1),
        w_dw=w_dw[0], b_dw=b_dw[0], ln_g=ln_g[0], ln_b=ln_b[0],
        w_o=w_o[0].astype(BF16),
        g_ffn=g_ffn[0],
        w_gate=jnp.pad(w_gate[0].astype(BF16), ((0, 0), (0, ff_pad))),
        w_up=jnp.pad(w_up[0].astype(BF16), ((0, 0), (0, ff_pad))),
        w_down=jnp.pad(w_down[0].astype(BF16), ((0, ff_pad), (0, 0))),
        g_final=g_final,
    )
    y_prompt = _encoder(x_prompt, 0, mod3, **weights)
    y_sample = _encoder(x_sample, n_prompt, mod3, **weights)
    return (y_prompt, y_sample)
```

```python
import functools

import jax
import jax.numpy as jnp
from jax import lax
from jax.experimental import pallas as pl
from jax.experimental.pallas import tpu as pltpu

F32 = jnp.float32
BF16 = jnp.bfloat16

D_MODEL = 4096
GRID_W = 64
D_ATTN = D_MODEL // 2
D_CONV = D_MODEL - D_ATTN
HEAD_DIM = 128
N_HEADS = D_ATTN // HEAD_DIM
WIN_ROWS = 8
WIN_COLS = 16
CONV_WIDTH = 31
CONV_PAD = CONV_WIDTH // 2
D_FF = 11008
N_MOD = 6
EPS = 1e-6
LN_EPS = 1e-5

LANES = 128
VMEM_BYTES_V7X = 64 * 1024 * 1024
VMEM_LIMIT = VMEM_BYTES_V7X - 8 * 1024 * 1024

N_RPB_ROWS = 2 * WIN_ROWS - 1
N_RPB_COLS = 2 * WIN_COLS - 1
N_BIAS_PAIRS = N_RPB_ROWS - 1
MASKED = -1e30
LOG2E = 1.4426950408889634
ATTN_GROUP = 16
HALO = 16
N_CCHUNK = D_CONV // LANES


def _params(sem, est_bytes):
    return pltpu.CompilerParams(dimension_semantics=sem, vmem_limit_bytes=VMEM_LIMIT)


def _row_tile(s, cap):
    t = min(s, cap)
    assert s % t == 0
    return t


def _mod_kernel(c_ref, w_ref, b_ref, o_ref):
    c = c_ref[...]
    s = (c * jax.nn.sigmoid(c)).astype(BF16)
    w = w_ref[...].astype(BF16)
    o_ref[...] = jnp.dot(s, w, preferred_element_type=F32) + b_ref[...]


def adaln_mod(c, w, b):
    nb, d = c.shape
    n = w.shape[1]
    tn = 512
    return pl.pallas_call(
        _mod_kernel,
        grid=(n // tn,),
        in_specs=[
            pl.BlockSpec((nb, d), lambda j: (0, 0)),
            pl.BlockSpec((d, tn), lambda j: (0, j)),
            pl.BlockSpec((1, tn), lambda j: (0, j)),
        ],
        out_specs=pl.BlockSpec((nb, tn), lambda j: (0, j)),
        out_shape=jax.ShapeDtypeStruct((nb, n), F32),
        compiler_params=_params(("parallel",), 0),
        name="adaln_mod",
    )(c, w, b.reshape(1, n))


def _prenorm_kernel(x_ref, g_ref, mod_ref, o_ref, *, shift_idx, scale_idx):
    x = x_ref[0]
    ms = jnp.mean(x * x, axis=-1, keepdims=True)
    y = (x * lax.rsqrt(ms + EPS)) * g_ref[...]
    if scale_idx is not None:
        y = y * (1.0 + mod_ref[0, scale_idx:scale_idx + 1, :]) + mod_ref[0, shift_idx:shift_idx + 1, :]
    o_ref[0] = y.astype(o_ref.dtype)


def prenorm(x, g, mod3, boff, shift_idx, scale_idx, out_dtype):
    b, s, d = x.shape
    tr = _row_tile(s, 256)
    return pl.pallas_call(
        functools.partial(_prenorm_kernel, shift_idx=shift_idx, scale_idx=scale_idx),
        grid=(b, s // tr),
        in_specs=[
            pl.BlockSpec((1, tr, d), lambda bi, i: (bi, i, 0)),
            pl.BlockSpec((1, d), lambda bi, i: (0, 0)),
            pl.BlockSpec((1, N_MOD, d), lambda bi, i: (bi + boff, 0, 0)),
        ],
        out_specs=pl.BlockSpec((1, tr, d), lambda bi, i: (bi, i, 0)),
        out_shape=jax.ShapeDtypeStruct((b, s, d), out_dtype),
        compiler_params=_params(("parallel", "parallel"), 0),
        name="prenorm",
    )(x, g.reshape(1, d), mod3)


def _mm_kernel(x_ref, w_ref, o_ref):
    o_ref[0] = jnp.dot(x_ref[0], w_ref[...], preferred_element_type=F32).astype(o_ref.dtype)


def mm_in(h, w):
    b, s, k = h.shape
    n = w.shape[1]
    tm = _row_tile(s, 1024)
    tn = 1024
    return pl.pallas_call(
        _mm_kernel,
        grid=(b, s // tm, n // tn),
        in_specs=[
            pl.BlockSpec((1, tm, k), lambda bi, i, j: (bi, i, 0)),
            pl.BlockSpec((k, tn), lambda bi, i, j: (0, j)),
        ],
        out_specs=pl.BlockSpec((1, tm, tn), lambda bi, i, j: (bi, i, j)),
        out_shape=jax.ShapeDtypeStruct((b, s, n), BF16),
        compiler_params=_params(("parallel", "parallel", "parallel"), 0),
        name="mm_in",
    )(h, w)


def _attn_kernel(rpb_ref, q_ref, k_ref, v_ref, o_ref, bias_scr, *, rows):
    @pl.when(pl.program_id(1) == 0)
    def _build_bias():
        base = pl.program_id(0) * (N_RPB_ROWS * N_RPB_COLS)
        lane = lax.broadcasted_iota(jnp.int32, (8, 2 * GRID_W), 1)
        sub = lax.broadcasted_iota(jnp.int32, (8, 2 * GRID_W), 0)
        kc = lane & (GRID_W - 1)
        for strip in range(GRID_W // 8):
            qc = sub + 8 * strip
            d = kc - qc + (WIN_COLS - 1)
            cs = jnp.clip(qc - WIN_COLS // 2, 0, GRID_W - WIN_COLS)
            in_win = (kc >= cs) & (kc < cs + WIN_COLS)
            tiles = [jnp.zeros((8, 2 * GRID_W), F32) for _ in range(N_RPB_ROWS)]
            for j in range(N_RPB_COLS):
                hit = d == j
                for i in range(N_RPB_ROWS):
                    tiles[i] = jnp.where(hit, rpb_ref[base + i * N_RPB_COLS + j], tiles[i])
            for i in range(N_BIAS_PAIRS):
                pair = jnp.where(lane < GRID_W, tiles[i], tiles[i + 1])
                bias_scr[i, 8 * strip:8 * strip + 8, :] = jnp.where(in_win, pair * LOG2E, MASKED)

    scale = HEAD_DIM ** -0.5 * LOG2E
    band = WIN_ROWS * GRID_W

    def body(g, carry):
        geo, scores, probs = [], [], []
        for u in range(ATTN_GROUP):
            r = g * ATTN_GROUP + u
            rs = jnp.clip(r - WIN_ROWS // 2, 0, rows - WIN_ROWS)
            first = (WIN_ROWS - 1) - (r - rs)
            q0 = pl.multiple_of(r * GRID_W, GRID_W)
            k0 = pl.multiple_of(rs * GRID_W, GRID_W)
            geo.append((q0, k0, first))
            q = q_ref[0, pl.ds(q0, GRID_W), :]
            kb = k_ref[0, pl.ds(k0, band), :]
            scores.append(lax.dot_general(q, kb, (((1,), (1,)), ((), ())), preferred_element_type=F32))
        for u in range(ATTN_GROUP):
            first = geo[u][2]
            bias = jnp.concatenate([bias_scr[first + 2 * m] for m in range(WIN_ROWS // 2)], axis=1)
            s = scores[u] * scale + bias
            m = jnp.max(s, axis=1, keepdims=True)
            p = jnp.exp2(s - m)
            probs.append((p.astype(BF16), jnp.sum(p, axis=1, keepdims=True)))
        for u in range(ATTN_GROUP):
            q0, k0, _ = geo[u]
            p, l = probs[u]
            o = jnp.dot(p, v_ref[0, pl.ds(k0, band), :], preferred_element_type=F32)
            o_ref[0, pl.ds(q0, GRID_W), :] = (o / l).astype(o_ref.dtype)
        return carry

    lax.fori_loop(0, rows // ATTN_GROUP, body, 0)


def nbr_attn(proj, rpb_flat):
    b, s, _ = proj.shape
    rows = s // GRID_W
    assert rows >= WIN_ROWS
    blk = (1, s, HEAD_DIM)
    return pl.pallas_call(
        functools.partial(_attn_kernel, rows=rows),
        grid=(N_HEADS, b),
        in_specs=[
            pl.BlockSpec(memory_space=pltpu.SMEM),
            pl.BlockSpec(blk, lambda h, bi: (bi, 0, h)),
            pl.BlockSpec(blk, lambda h, bi: (bi, 0, N_HEADS + h)),
            pl.BlockSpec(blk, lambda h, bi: (bi, 0, 2 * N_HEADS + h)),
        ],
        out_specs=pl.BlockSpec(blk, lambda h, bi: (bi, 0, h)),
        out_shape=jax.ShapeDtypeStruct((b, s, D_ATTN + D_CONV), BF16),
        scratch_shapes=[pltpu.VMEM((N_BIAS_PAIRS, GRID_W, 2 * GRID_W), F32)],
        compiler_params=_params(("parallel", "arbitrary"), 0),
        name="nbr_attn",
    )(rpb_flat, proj, proj, proj)


def _conv_kernel(a_ref, ap_ref, an_ref, g_ref, gp_ref, gn_ref, w_ref, bdw_ref, lng_ref, lnb_ref, ycat_ref,
                 o_ref, u_scr, c_scr, *, t_rows, n_blk, row_chunk):
    del ycat_ref
    i = pl.program_id(1)

    def glu(a, g):
        return a.astype(F32) * jax.nn.sigmoid(g.astype(F32))

    keep_prev = (i > 0).astype(F32)
    keep_next = (i < n_blk - 1).astype(F32)
    for c in range(N_CCHUNK):
        cols = slice(c * LANES, (c + 1) * LANES)
        u_scr[c, 0:HALO, :] = glu(ap_ref[0, :, cols], gp_ref[0, :, cols]) * keep_prev
        u_scr[c, HALO:HALO + t_rows, :] = glu(a_ref[0, :, cols], g_ref[0, :, cols])
        u_scr[c, HALO + t_rows:2 * HALO + t_rows, :] = glu(an_ref[0, :, cols], gn_ref[0, :, cols]) * keep_next

    def conv_chunk(c, carry):
        w = w_ref[c]
        bias = bdw_ref[c]
        for rc in range(t_rows // row_chunk):
            acc = jnp.zeros((row_chunk, LANES), F32)
            for j in range(CONV_WIDTH):
                start = rc * row_chunk + HALO - CONV_PAD + j
                acc = acc + u_scr[c, start:start + row_chunk, :] * w[j:j + 1, :]
            c_scr[c, rc * row_chunk:(rc + 1) * row_chunk, :] = acc + bias
        return carry

    lax.fori_loop(0, N_CCHUNK, conv_chunk, 0)

    tot = c_scr[0]
    for c in range(1, N_CCHUNK):
        tot = tot + c_scr[c]
    mu = jnp.sum(tot, axis=1, keepdims=True) * (1.0 / D_CONV)
    sq = jnp.zeros((t_rows, LANES), F32)
    for c in range(N_CCHUNK):
        dlt = c_scr[c] - mu
        sq = sq + dlt * dlt
    var = jnp.sum(sq, axis=1, keepdims=True) * (1.0 / D_CONV)
    rstd = lax.rsqrt(var + LN_EPS)
    for c in range(N_CCHUNK):
        yn = (c_scr[c] - mu) * rstd * lng_ref[c] + lnb_ref[c]
        o_ref[0, :, c * LANES:(c + 1) * LANES] = (yn * jax.nn.sigmoid(yn)).astype(o_ref.dtype)


def conv_glu_ln(proj, ycat, w_dw, b_dw, ln_g, ln_b):
    b, s, _ = proj.shape
    t_rows = _row_tile(s, 256)
    n_blk = s // t_rows
    hpb = t_rows // HALO
    a_col = 3 * D_ATTN // D_CONV
    g_col = a_col + 1
    main = (1, t_rows, D_CONV)
    halo = (1, HALO, D_CONV)

    def prev_map(col):
        return lambda bi, i: (bi, jnp.maximum(i * hpb - 1, 0), col)

    def next_map(col):
        return lambda bi, i: (bi, jnp.minimum((i + 1) * hpb, s // HALO - 1), col)

    w_chunks = jnp.pad(w_dw, ((0, 1), (0, 0))).reshape(CONV_WIDTH + 1, N_CCHUNK, LANES).transpose(1, 0, 2)
    per_chunk = lambda v: v.reshape(N_CCHUNK, 1, LANES)
    small = pl.BlockSpec((N_CCHUNK, 1, LANES), lambda bi, i: (0, 0, 0))
    return pl.pallas_call(
        functools.partial(_conv_kernel, t_rows=t_rows, n_blk=n_blk, row_chunk=64),
        grid=(b, n_blk),
        in_specs=[
            pl.BlockSpec(main, lambda bi, i: (bi, i, a_col)),
            pl.BlockSpec(halo, prev_map(a_col)),
            pl.BlockSpec(halo, next_map(a_col)),
            pl.BlockSpec(main, lambda bi, i: (bi, i, g_col)),
            pl.BlockSpec(halo, prev_map(g_col)),
            pl.BlockSpec(halo, next_map(g_col)),
            pl.BlockSpec((N_CCHUNK, CONV_WIDTH + 1, LANES), lambda bi, i: (0, 0, 0)),
            small, small, small,
            pl.BlockSpec(memory_space=pl.ANY),
        ],
        out_specs=pl.BlockSpec(main, lambda bi, i: (bi, i, D_ATTN // D_CONV)),
        out_shape=jax.ShapeDtypeStruct(ycat.shape, ycat.dtype),
        scratch_shapes=[
            pltpu.VMEM((N_CCHUNK, t_rows + 2 * HALO, LANES), F32),
            pltpu.VMEM((N_CCHUNK, t_rows, LANES), F32),
        ],
        input_output_aliases={10: 0},
        compiler_params=_params(("parallel", "parallel"), 0),
        name="conv_glu_ln",
    )(proj, proj, proj, proj, proj, proj, w_chunks, per_chunk(b_dw), per_chunk(ln_g), per_chunk(ln_b), ycat)


def _mm_res_kernel(a_ref, w_ref, x_ref, mod_ref, o_ref, *acc, gate_idx, nk):
    part = jnp.dot(a_ref[0], w_ref[...], preferred_element_type=F32)
    gate = mod_ref[0, gate_idx:gate_idx + 1, :]
    if nk == 1:
        o_ref[0] = x_ref[0] + gate * part
        return
    (acc_scr,) = acc
    k = pl.program_id(3)

    @pl.when(k == 0)
    def _():
        acc_scr[...] = part

    @pl.when((k > 0) & (k < nk - 1))
    def _():
        acc_scr[...] += part

    @pl.when(k == nk - 1)
    def _():
        o_ref[0] = x_ref[0] + gate * (acc_scr[...] + part)


def mm_res(a, w, x, mod3, boff, gate_idx, tk):
    b, s, kd = a.shape
    n = w.shape[1]
    tm = _row_tile(s, 1024)
    tn = 512
    assert kd % tk == 0 and tk % LANES == 0
    nk = kd // tk
    return pl.pallas_call(
        functools.partial(_mm_res_kernel, gate_idx=gate_idx, nk=nk),
        grid=(b, s // tm, n // tn, nk),
        in_specs=[
            pl.BlockSpec((1, tm, tk), lambda bi, i, j, k: (bi, i, k)),
            pl.BlockSpec((tk, tn), lambda bi, i, j, k: (k, j)),
            pl.BlockSpec((1, tm, tn), lambda bi, i, j, k: (bi, i, j)),
            pl.BlockSpec((1, N_MOD, tn), lambda bi, i, j, k: (bi + boff, 0, j)),
        ],
        out_specs=pl.BlockSpec((1, tm, tn), lambda bi, i, j, k: (bi, i, j)),
        out_shape=jax.ShapeDtypeStruct((b, s, n), F32),
        scratch_shapes=[pltpu.VMEM((tm, tn), F32)] if nk > 1 else [],
        compiler_params=_params(("parallel", "parallel", "parallel", "arbitrary"), 0),
        name="mm_res",
    )(a, w, x, mod3)


def _ffn_up_kernel(h_ref, wg_ref, wu_ref, o_ref):
    h = h_ref[0]
    g = jnp.dot(h, wg_ref[...], preferred_element_type=F32)
    u = jnp.dot(h, wu_ref[...], preferred_element_type=F32)
    o_ref[0] = ((g * jax.nn.sigmoid(g)) * u).astype(o_ref.dtype)


def ffn_up(h, wg, wu):
    b, s, k = h.shape
    n = wg.shape[1]
    tm = _row_tile(s, 2048)
    tn = 256
    assert n % tn == 0
    return pl.pallas_call(
        _ffn_up_kernel,
        grid=(b, s // tm, n // tn),
        in_specs=[
            pl.BlockSpec((1, tm, k), lambda bi, i, j: (bi, i, 0)),
            pl.BlockSpec((k, tn), lambda bi, i, j: (0, j)),
            pl.BlockSpec((k, tn), lambda bi, i, j: (0, j)),
        ],
        out_specs=pl.BlockSpec((1, tm, tn), lambda bi, i, j: (bi, i, j)),
        out_shape=jax.ShapeDtypeStruct((b, s, n), BF16),
        compiler_params=_params(("parallel", "parallel", "parallel"), 0),
        name="ffn_up",
    )(h, wg, wu)


def _encoder(x, boff, mod3, g_mix, w_in, rpb_flat, w_dw, b_dw, ln_g, ln_b, w_o, g_ffn, w_gate, w_up, w_down,
             g_final):
    h = prenorm(x, g_mix, mod3, boff, 0, 1, BF16)
    proj = mm_in(h, w_in)
    ycat = nbr_attn(proj, rpb_flat)
    ycat = conv_glu_ln(proj, ycat, w_dw, b_dw, ln_g, ln_b)
    x = mm_res(ycat, w_o, x, mod3, boff, 2, tk=D_ATTN + D_CONV)
    h = prenorm(x, g_ffn, mod3, boff, 3, 4, BF16)
    act = ffn_up(h, w_gate, w_up)
    x = mm_res(act, w_down, x, mod3, boff, 5, tk=D_FF // 2)
    return prenorm(x, g_final, mod3, boff, None, None, F32)


def kernel(x_prompt, x_sample, c_prompt, c_sample, w_ada, b_ada, g_mix, w_in, rpb, w_dw, b_dw, ln_g, ln_b, w_o,
           g_ffn, w_gate, w_up, w_down, g_final):
    assert w_ada.shape[0] == 1, "single-layer block"
    n_prompt = c_prompt.shape[0]
    c_all = jnp.concatenate([c_prompt, c_sample], axis=0)
    mod = adaln_mod(c_all, w_ada[0], b_ada[0])
    mod3 = mod.reshape(c_all.shape[0], N_MOD, D_MODEL)

    weights = dict(
        g_mix=g_mix[0],
        w_in=w_in[0].astype(BF16),
        rpb_flat=rpb[0].reshape(-1),
        w_dw=w_dw[0], b_dw=b_dw[0], ln_g=ln_g[0], ln_b=ln_b[0],
        w_o=w_o[0].astype(BF16),
        g_ffn=g_ffn[0],
        w_gate=w_gate[0].astype(BF16),
        w_up=w_up[0].astype(BF16),
        w_down=w_down[0].astype(BF16),
        g_final=g_final,
    )
    y_prompt = _encoder(x_prompt, 0, mod3, **weights)
    y_sample = _encoder(x_sample, n_prompt, mod3, **weights)
    return (y_prompt, y_sample)
```
